```python
import jax, jax.numpy as jnp
from jax import lax
import numpy as np

D_MODEL = 1024
BATCH = 16
SEQ = 4096
DEPTH = 1

CHUNK = 64
D_MIX = D_MODEL
SB_WIDTH = D_MIX // 2
SB_HEADS = 8
SB_HEAD_DIM = SB_WIDTH // SB_HEADS
SB_QBLOCK = 128
ML_WIDTH = D_MIX - SB_WIDTH
ML_HEADS = 4
ML_HEAD_DIM = ML_WIDTH // ML_HEADS
CONV_K = 4
N_IN = 4 * SB_WIDTH + 5 * ML_WIDTH + 2 * ML_HEADS
EPS = 1e-6
GATE_SCALE = 0.1

kernel_name = "hymba_stickbreaking_mlstm_block"


def rms_norm(x, g):
    xf = x.astype(jnp.float32)
    y = xf * lax.rsqrt(jnp.mean(xf * xf, axis=-1, keepdims=True) + EPS)
    return (y * g.astype(jnp.float32)).astype(x.dtype)


def head_layer_norm(h, g):
    mu = jnp.mean(h, axis=-1, keepdims=True)
    hc = h - mu
    var = jnp.mean(hc * hc, axis=-1, keepdims=True)
    y = hc * lax.rsqrt(var + EPS)
    return y * g.astype(jnp.float32).reshape(h.shape[2], h.shape[3])


def causal_depthwise_conv(u, w, b):
    k = w.shape[0]
    y = lax.conv_general_dilated(
        u, w[:, None, :].astype(u.dtype), window_strides=(1,), padding=[(k - 1, 0)],
        dimension_numbers=("NWC", "WIO", "NWC"), feature_group_count=u.shape[-1])
    return y + b.astype(u.dtype)


def stick_breaking_attention(q, k, v):
    s_len, d = q.shape[2], q.shape[3]
    scale = d ** -0.5
    qf, kf, vf = q.astype(jnp.float32), k.astype(jnp.float32), v.astype(jnp.float32)
    outs = []
    for blk in range(s_len // SB_QBLOCK):
        q0 = blk * SB_QBLOCK
        kl = q0 + SB_QBLOCK
        qb = qf[:, :, q0:kl]
        kc = kf[:, :, :kl]
        vc = vf[:, :, :kl]
        z = jnp.einsum("bhqd,bhkd->bhqk", qb, kc) * scale
        qpos = q0 + jnp.arange(SB_QBLOCK)
        kpos = jnp.arange(kl)
        strict = kpos[None, :] < qpos[:, None]
        log_one_minus = jnp.where(strict, jax.nn.log_sigmoid(-z), 0.0)
        suffix = lax.cumsum(log_one_minus, axis=3, reverse=True) - log_one_minus
        a = jnp.where(strict, jnp.exp(jax.nn.log_sigmoid(z) + suffix), 0.0)
        outs.append(jnp.einsum("bhqk,bhkd->bhqd", a, vc))
    return jnp.concatenate(outs, axis=2)


def mlstm_chunkwise(q, k, v, i_pre, f_pre):
    bsz, s_len, nh, dk = q.shape
    dv = v.shape[-1]
    nc = s_len // CHUNK

    def to_chunks(t):
        return t.reshape(bsz, nc, CHUNK, nh, -1).transpose(0, 3, 1, 2, 4)

    def gate_chunks(t):
        return t.reshape(bsz, nc, CHUNK, nh).transpose(0, 3, 1, 2)

    qc = to_chunks(q)
    kc = to_chunks(k * (dk ** -0.5))
    vc = to_chunks(v)
    ig = gate_chunks(i_pre)
    logf = jax.nn.log_sigmoid(gate_chunks(f_pre))
    b = jnp.cumsum(logf, axis=-1)
    g = b[..., -1]
    logw = g[..., None] - b + ig

    def step(carry, xs):
        c_st, n_st, m_st = carry
        k_c, v_c, logw_c, g_c = xs
        m_new = jnp.maximum(g_c + m_st, jnp.max(logw_c, axis=-1))
        w = jnp.exp(logw_c - m_new[..., None])
        decay = jnp.exp(g_c + m_st - m_new)
        c_new = decay[..., None, None] * c_st + jnp.einsum("bhl,bhlk,bhlv->bhkv", w, k_c, v_c)
        n_new = decay[..., None] * n_st + jnp.einsum("bhl,bhlk->bhk", w, k_c)
        return (c_new, n_new, m_new), (c_st, n_st, m_st)

    init = (jnp.zeros((bsz, nh, dk, dv), jnp.float32),
            jnp.zeros((bsz, nh, dk), jnp.float32),
            jnp.zeros((bsz, nh), jnp.float32))
    xs = (kc.transpose(2, 0, 1, 3, 4), vc.transpose(2, 0, 1, 3, 4),
          logw.transpose(2, 0, 1, 3), g.transpose(2, 0, 1))
    _, (c_all, n_all, m_all) = lax.scan(step, init, xs)
    c_all = c_all.transpose(1, 2, 0, 3, 4)
    n_all = n_all.transpose(1, 2, 0, 3)
    m_all = m_all.transpose(1, 2, 0)

    causal = jnp.tril(jnp.ones((CHUNK, CHUNK), dtype=bool))
    dmat = b[..., :, None] - b[..., None, :] + ig[..., None, :]
    dmat = jnp.where(causal, dmat, -jnp.inf)
    m_inter = b + m_all[..., None]
    m_out = jnp.maximum(m_inter, jnp.max(dmat, axis=-1))
    scores = jnp.einsum("bhcld,bhcsd->bhcls", qc, kc) * jnp.exp(dmat - m_out[..., None])
    inter_w = jnp.exp(m_inter - m_out)
    num = (inter_w[..., None] * jnp.einsum("bhcld,bhcdv->bhclv", qc, c_all)
           + jnp.einsum("bhcls,bhcsv->bhclv", scores, vc))
    den = inter_w * jnp.einsum("bhcld,bhcd->bhcl", qc, n_all) + jnp.sum(scores, axis=-1)
    h = num / jnp.maximum(jnp.abs(den), jnp.exp(-m_out))[..., None]
    return h.transpose(0, 2, 3, 1, 4).reshape(bsz, s_len, nh, dv)


def setup_inputs(seed: int = 0) -> dict:
    key = jax.random.key(seed)
    ks = jax.random.split(key, 12)
    x = jax.random.normal(ks[0], (BATCH, SEQ, D_MODEL), jnp.float32)
    norm_g = 1.0 + 0.01 * jax.random.normal(ks[1], (DEPTH, D_MODEL), jnp.float32)
    col_scale = jnp.concatenate([jnp.ones((N_IN - 2 * ML_HEADS,), jnp.float32),
                                 jnp.full((2 * ML_HEADS,), GATE_SCALE, jnp.float32)])
    w_in = jax.random.normal(ks[2], (DEPTH, D_MODEL, N_IN), jnp.float32) * (D_MODEL ** -0.5) * col_scale
    b_igate = 0.1 * jax.random.normal(ks[3], (DEPTH, ML_HEADS), jnp.float32)
    b_fgate = (jnp.linspace(3.0, 6.0, ML_HEADS, dtype=jnp.float32)[None, :]
               + 0.01 * jax.random.normal(ks[4], (DEPTH, ML_HEADS), jnp.float32))
    conv_w = jax.random.normal(ks[5], (DEPTH, CONV_K, 2 * ML_WIDTH), jnp.float32) * (CONV_K ** -0.5)
    conv_b = 0.01 * jax.random.normal(ks[6], (DEPTH, 2 * ML_WIDTH), jnp.float32)
    head_norm_g = 1.0 + 0.01 * jax.random.normal(ks[7], (DEPTH, ML_WIDTH), jnp.float32)
    w_out = jax.random.normal(ks[8], (DEPTH, D_MIX, D_MODEL), jnp.float32) * (D_MIX ** -0.5)
    final_norm_g = 1.0 + 0.01 * jax.random.normal(ks[9], (D_MODEL,), jnp.float32)
    return {"x": x, "norm_g": norm_g, "w_in": w_in, "b_igate": b_igate, "b_fgate": b_fgate,
            "conv_w": conv_w, "conv_b": conv_b, "head_norm_g": head_norm_g,
            "w_out": w_out, "final_norm_g": final_norm_g}


def reference(x, norm_g, w_in, b_igate, b_fgate, conv_w, conv_b, head_norm_g, w_out, final_norm_g):
    bsz, s_len, _ = x.shape
    sizes = [SB_WIDTH] * 4 + [ML_WIDTH] * 5 + [ML_HEADS, ML_HEADS]
    split_at = [int(v) for v in np.cumsum(sizes)[:-1]]
    h = x
    for layer in range(DEPTH):
        u = rms_norm(h, norm_g[layer])
        p = u @ w_in[layer].astype(u.dtype)
        (sb_q, sb_k, sb_v, sb_z, m_q, m_k, m_v, m_o, m_z, m_i, m_f) = jnp.split(p, split_at, axis=-1)

        def sb_heads(t):
            return t.reshape(bsz, s_len, SB_HEADS, SB_HEAD_DIM).transpose(0, 2, 1, 3)
        sb = stick_breaking_attention(sb_heads(sb_q), sb_heads(sb_k), sb_heads(sb_v))
        sb = sb.transpose(0, 2, 1, 3).reshape(bsz, s_len, SB_WIDTH).astype(x.dtype)
        sb_out = sb * jax.nn.silu(sb_z)

        qk = causal_depthwise_conv(jnp.concatenate([m_q, m_k], axis=-1), conv_w[layer], conv_b[layer])
        mq, mk = jnp.split(qk, 2, axis=-1)

        def ml_heads(t):
            return t.reshape(bsz, s_len, ML_HEADS, ML_HEAD_DIM).astype(jnp.float32)
        i_pre = (m_i + b_igate[layer].astype(m_i.dtype)).astype(jnp.float32)
        f_pre = (m_f + b_fgate[layer].astype(m_f.dtype)).astype(jnp.float32)
        hm = mlstm_chunkwise(ml_heads(mq), ml_heads(mk), ml_heads(m_v), i_pre, f_pre)
        hm = hm * jax.nn.sigmoid(ml_heads(m_o))
        hm = head_layer_norm(hm, head_norm_g[layer])
        ml_out = hm.reshape(bsz, s_len, ML_WIDTH).astype(x.dtype) * jax.nn.silu(m_z)

        mix = jnp.concatenate([sb_out, ml_out], axis=-1)
        h = h + mix @ w_out[layer].astype(mix.dtype)
    return rms_norm(h, final_norm_g)
```

```python
import functools
import math

import jax
import jax.numpy as jnp
from jax import lax
from jax.experimental import pallas as pl
from jax.experimental.pallas import tpu as pltpu

SB_HEADS = 8
SB_HEAD_DIM = 64
ML_HEADS = 4
ML_HEAD_DIM = 128
CONV_K = 4
EPS = 1e-6

LANES = 128
SUBLANES = 8
VMEM_LIMIT_BYTES = 56 * 1024 * 1024

ROW_TILE = 512
SB_TQ = 128
SB_TK = 256
ML_CHUNK = 128

_NT = (((1,), (1,)), ((), ()))
_TN = (((0,), (0,)), ((), ()))


def _split3(x):
    hi = x.astype(jnp.bfloat16)
    r1 = x - hi.astype(jnp.float32)
    mid = r1.astype(jnp.bfloat16)
    r2 = r1 - mid.astype(jnp.float32)
    lo = r2.astype(jnp.bfloat16)
    return hi, mid, lo


def _log_sigmoid(x):
    return jnp.minimum(x, 0.0) - jnp.log1p(jnp.exp(-jnp.abs(x)))


def _in_proj_kernel(x_ref, g_ref, w_ref, gb_ref, cw_ref, cb_ref,
                    sbq_ref, sbk_ref, sbv_ref, sbz_ref,
                    mq_ref, mk_ref, mv_ref, mo_ref, mz_ref, gate_ref,
                    conv_buf, *, sb_w, ml_w, tm):
    i = pl.program_id(1)
    x = x_ref[...]
    ms = jnp.mean(x * x, axis=-1, keepdims=True)
    u = (x * lax.rsqrt(ms + EPS) * g_ref[...]).astype(jnp.bfloat16)

    def proj(lo, width):
        return jnp.dot(u, w_ref[:, lo:lo + width], preferred_element_type=jnp.float32)

    sbq_ref[...] = (proj(0, sb_w) * (SB_HEAD_DIM ** -0.5)).astype(jnp.bfloat16)
    sbk_ref[...] = proj(sb_w, sb_w).astype(jnp.bfloat16)
    sbv_ref[...] = proj(2 * sb_w, sb_w).astype(jnp.bfloat16)
    sbz_ref[...] = proj(3 * sb_w, sb_w)
    base = 4 * sb_w
    mv_ref[...] = proj(base + 2 * ml_w, ml_w).astype(jnp.bfloat16)
    mo_ref[...] = proj(base + 3 * ml_w, ml_w)
    mz_ref[...] = proj(base + 4 * ml_w, ml_w)
    gate_ref[...] = proj(base + 5 * ml_w, LANES) + gb_ref[...]

    @pl.when(i == 0)
    def _():
        conv_buf[0:SUBLANES, :] = jnp.zeros((SUBLANES, 2 * ml_w), jnp.float32)

    conv_buf[SUBLANES:SUBLANES + tm, :] = proj(base, 2 * ml_w)
    acc = jnp.broadcast_to(cb_ref[...], (tm, 2 * ml_w))
    for j in range(CONV_K):
        off = SUBLANES - (CONV_K - 1) + j
        acc = acc + cw_ref[j:j + 1, :] * conv_buf[off:off + tm, :]
    conv_buf[0:SUBLANES, :] = conv_buf[tm:tm + SUBLANES, :]
    mq_ref[...] = acc[:, :ml_w].astype(jnp.bfloat16)
    mk_ref[...] = (acc[:, ml_w:] * (ML_HEAD_DIM ** -0.5)).astype(jnp.bfloat16)


def _in_proj(x, norm_g, w_pad, gate_bias, conv_w, conv_b, *, sb_w, ml_w):
    bsz, s_len, d_model = x.shape
    tm = min(ROW_TILE, s_len)
    n_pad = w_pad.shape[1]
    grid = (bsz, s_len // tm)
    row = lambda width: pl.BlockSpec((None, tm, width), lambda b, i: (b, i, 0))
    const = lambda shape: pl.BlockSpec(shape, lambda b, i: (0,) * len(shape))
    bf = lambda width: jax.ShapeDtypeStruct((bsz, s_len, width), jnp.bfloat16)
    f32 = lambda width: jax.ShapeDtypeStruct((bsz, s_len, width), jnp.float32)
    return pl.pallas_call(
        functools.partial(_in_proj_kernel, sb_w=sb_w, ml_w=ml_w, tm=tm),
        grid=grid,
        in_specs=[row(d_model), const((1, d_model)), const((d_model, n_pad)),
                  const((1, LANES)), const((CONV_K, 2 * ml_w)), const((1, 2 * ml_w))],
        out_specs=[row(sb_w), row(sb_w), row(sb_w), row(sb_w),
                   row(ml_w), row(ml_w), row(ml_w), row(ml_w), row(ml_w), row(LANES)],
        out_shape=[bf(sb_w), bf(sb_w), bf(sb_w), f32(sb_w),
                   bf(ml_w), bf(ml_w), bf(ml_w), f32(ml_w), f32(ml_w), f32(LANES)],
        scratch_shapes=[pltpu.VMEM((tm + SUBLANES, 2 * ml_w), jnp.float32)],
        compiler_params=pltpu.CompilerParams(
            dimension_semantics=("arbitrary", "arbitrary"),
            vmem_limit_bytes=VMEM_LIMIT_BYTES),
        name="in_proj",
    )(x, norm_g, w_pad, gate_bias, conv_w, conv_b)


def _sb_tile(q_a, q_b, k_t, v_t, tri, carry_a, carry_b, strict):
    lane = lax.broadcasted_iota(jnp.int32, (1, LANES), 1)
    outs = []
    carries = []
    for q_h, carry in ((q_a, carry_a), (q_b, carry_b)):
        z = lax.dot_general(q_h, k_t, _NT, preferred_element_type=jnp.float32)
        lg = jnp.log1p(jnp.exp(-jnp.abs(z)))
        ls = jnp.minimum(z, 0.0) - lg
        l1m = ls - z
        if strict is not None:
            l1m = jnp.where(strict, l1m, 0.0)
        hi = l1m.astype(jnp.bfloat16)
        lo = (l1m - hi.astype(jnp.float32)).astype(jnp.bfloat16)
        suf = jnp.dot(jnp.concatenate([hi, lo], axis=1), tri,
                      preferred_element_type=jnp.float32)
        a = jnp.exp(ls + suf)
        if strict is not None:
            a = jnp.where(strict, a, 0.0)
        o = jnp.dot(a.astype(jnp.bfloat16), v_t, preferred_element_type=jnp.float32)
        outs.append(o * jnp.exp(carry))
        carries.append(carry + suf[:, 0:1] + l1m[:, 0:1])
    out = jnp.where(lane < SB_HEAD_DIM, outs[0], outs[1])
    return out, carries[0], carries[1]


def _sb_attn_kernel(q_ref, k_ref, v_ref, z_ref, tri_ref, o_ref, *, tq, tk):
    qi = pl.program_id(2)
    lane = lax.broadcasted_iota(jnp.int32, (1, LANES), 1)
    q = q_ref[...]
    zero = jnp.zeros_like(q)
    q_a = jnp.where(lane < SB_HEAD_DIM, q, zero)
    q_b = jnp.where(lane < SB_HEAD_DIM, zero, q)
    tri = tri_ref[...]

    q0 = qi * tq
    jd = q0 // tk
    k0 = pl.multiple_of(jd * tk, tk)
    qpos = q0 + lax.broadcasted_iota(jnp.int32, (tq, tk), 0)
    kpos = k0 + lax.broadcasted_iota(jnp.int32, (tq, tk), 1)
    strict = kpos < qpos
    zc = jnp.zeros((tq, 1), jnp.float32)
    acc, carry_a, carry_b = _sb_tile(
        q_a, q_b, k_ref[pl.ds(k0, tk), :], v_ref[pl.ds(k0, tk), :], tri, zc, zc, strict)

    def body(step, state):
        acc, carry_a, carry_b = state
        ks = pl.multiple_of((jd - 1 - step) * tk, tk)
        out, carry_a, carry_b = _sb_tile(
            q_a, q_b, k_ref[pl.ds(ks, tk), :], v_ref[pl.ds(ks, tk), :], tri,
            carry_a, carry_b, None)
        return acc + out, carry_a, carry_b

    acc, _, _ = lax.fori_loop(0, jd, body, (acc, carry_a, carry_b))
    gate = z_ref[...]
    o_ref[...] = (acc * (gate * jax.nn.sigmoid(gate))).astype(o_ref.dtype)


def _sb_attn(sbq, sbk, sbv, sbz):
    bsz, s_len, sb_w = sbq.shape
    tq = min(SB_TQ, s_len)
    tk = min(SB_TK, s_len)
    n_pairs = sb_w // LANES
    j = lax.broadcasted_iota(jnp.int32, (tk, tk), 0)
    s = lax.broadcasted_iota(jnp.int32, (tk, tk), 1)
    tri = (j > s).astype(jnp.bfloat16)
    tri2 = jnp.concatenate([tri, tri], axis=0)
    q_spec = pl.BlockSpec((None, tq, LANES), lambda b, p, i: (b, i, p))
    kv_spec = pl.BlockSpec((None, s_len, LANES), lambda b, p, i: (b, 0, p))
    return pl.pallas_call(
        functools.partial(_sb_attn_kernel, tq=tq, tk=tk),
        grid=(bsz, n_pairs, s_len // tq),
        in_specs=[q_spec, kv_spec, kv_spec, q_spec,
                  pl.BlockSpec((2 * tk, tk), lambda b, p, i: (0, 0))],
        out_specs=q_spec,
        out_shape=jax.ShapeDtypeStruct((bsz, s_len, sb_w), jnp.bfloat16),
        compiler_params=pltpu.CompilerParams(
            dimension_semantics=("arbitrary", "arbitrary", "arbitrary"),
            vmem_limit_bytes=VMEM_LIMIT_BYTES),
        name="sb_attn",
    )(sbq, sbk, sbv, sbz, tri2)


def _mlstm_kernel(q_ref, k_ref, v_ref, o_ref, z_ref, gate_ref, hg_ref, out_ref,
                  c_sc, n_sc, m_sc, *, chunk):
    ci = pl.program_id(1)

    @pl.when(ci == 0)
    def _():
        c_sc[...] = jnp.zeros_like(c_sc)
        n_sc[...] = jnp.zeros_like(n_sc)
        m_sc[...] = jnp.zeros_like(m_sc)

    row = lax.broadcasted_iota(jnp.int32, (chunk, chunk), 0)
    col = lax.broadcasted_iota(jnp.int32, (chunk, chunk), 1)
    causal = col <= row
    tri_incl = causal.astype(jnp.bfloat16)
    ones = jnp.ones((chunk, LANES), jnp.bfloat16)
    lane = lax.broadcasted_iota(jnp.int32, (1, LANES), 1)
    gates = gate_ref[...]

    for h in range(ML_HEADS):
        hs = slice(h * ML_HEAD_DIM, (h + 1) * ML_HEAD_DIM)
        ig = jnp.broadcast_to(gates[:, h:h + 1], (chunk, LANES))
        lf = _log_sigmoid(jnp.broadcast_to(gates[:, ML_HEADS + h:ML_HEADS + h + 1],
                                           (chunk, LANES)))
        b = sum(jnp.dot(tri_incl, part, preferred_element_type=jnp.float32)
                for part in _split3(lf))
        g = b[chunk - 1:chunk, :]
        m_st = m_sc[h:h + 1, :]
        n_st = n_sc[h:h + 1, :]
        c_st = c_sc[h]

        q = q_ref[:, hs]
        k = k_ref[:, hs]
        v = v_ref[:, hs]

        cvec = jnp.where(lane == 0, ig - b, 0.0)
        r = sum(lax.dot_general(ones, part, _NT, preferred_element_type=jnp.float32)
                for part in _split3(cvec))
        dmat = jnp.where(causal, b[:, :chunk] + r, -jnp.inf)
        m_intra = jnp.max(dmat, axis=-1, keepdims=True)
        m_inter = b + m_st
        m_out = jnp.maximum(m_inter, m_intra)
        qk = lax.dot_general(q, k, _NT, preferred_element_type=jnp.float32)
        scores = qk * jnp.exp(dmat - m_out[:, :chunk])
        inter_w = jnp.exp(m_inter - m_out)
        qf = q.astype(jnp.float32)
        num = (inter_w * jnp.dot(q, c_st.astype(jnp.bfloat16),
                                 preferred_element_type=jnp.float32)
               + jnp.dot(scores.astype(jnp.bfloat16), v, preferred_element_type=jnp.float32))
        den = (jnp.sum(inter_w * qf * n_st, axis=-1, keepdims=True)
               + jnp.sum(scores, axis=-1, keepdims=True))
        hm = num / jnp.maximum(jnp.abs(den), jnp.exp(-m_out))

        logw = g - b + ig
        m_new = jnp.maximum(g + m_st, jnp.max(logw, axis=0, keepdims=True))
        wk = k.astype(jnp.float32) * jnp.exp(logw - m_new)
        decay = jnp.exp(g + m_st - m_new)
        c_sc[h] = decay * c_st + lax.dot_general(
            wk.astype(jnp.bfloat16), v, _TN, preferred_element_type=jnp.float32)
        n_sc[h:h + 1, :] = decay * n_st + jnp.sum(wk, axis=0, keepdims=True)
        m_sc[h:h + 1, :] = m_new

        hm = hm * jax.nn.sigmoid(o_ref[:, hs])
        mu = jnp.mean(hm, axis=-1, keepdims=True)
        hc = hm - mu
        var = jnp.mean(hc * hc, axis=-1, keepdims=True)
        y = hc * lax.rsqrt(var + EPS) * hg_ref[:, hs]
        zg = z_ref[:, hs]
        out_ref[:, hs] = (y * (zg * jax.nn.sigmoid(zg))).astype(out_ref.dtype)


def _mlstm(mq, mk, mv, mo, mz, gates, head_norm_g):
    bsz, s_len, ml_w = mq.shape
    chunk = min(ML_CHUNK, s_len)
    row = lambda width: pl.BlockSpec((None, chunk, width), lambda b, c: (b, c, 0))
    return pl.pallas_call(
        functools.partial(_mlstm_kernel, chunk=chunk),
        grid=(bsz, s_len // chunk),
        in_specs=[row(ml_w), row(ml_w), row(ml_w), row(ml_w), row(ml_w), row(LANES),
                  pl.BlockSpec((1, ml_w), lambda b, c: (0, 0))],
        out_specs=row(ml_w),
        out_shape=jax.ShapeDtypeStruct((bsz, s_len, ml_w), jnp.bfloat16),
        scratch_shapes=[pltpu.VMEM((ML_HEADS, ML_HEAD_DIM, ML_HEAD_DIM), jnp.float32),
                        pltpu.VMEM((SUBLANES, LANES), jnp.float32),
                        pltpu.VMEM((SUBLANES, LANES), jnp.float32)],
        compiler_params=pltpu.CompilerParams(
            dimension_semantics=("arbitrary", "arbitrary"),
            vmem_limit_bytes=VMEM_LIMIT_BYTES),
        name="mlstm",
    )(mq, mk, mv, mo, mz, gates, head_norm_g)


def _out_proj_kernel(x_ref, sb_ref, ml_ref, wa_ref, wb_ref, fg_ref, o_ref, *, final_norm):
    y = (x_ref[...]
         + jnp.dot(sb_ref[...], wa_ref[...], preferred_element_type=jnp.float32)
         + jnp.dot(ml_ref[...], wb_ref[...], preferred_element_type=jnp.float32))
    if final_norm:
        ms = jnp.mean(y * y, axis=-1, keepdims=True)
        y = y * lax.rsqrt(ms + EPS) * fg_ref[...]
    o_ref[...] = y


def _out_proj(x, sb_out, ml_out, w_a, w_b, final_g, *, final_norm):
    bsz, s_len, d_model = x.shape
    tm = min(ROW_TILE, s_len)
    row = lambda width: pl.BlockSpec((None, tm, width), lambda b, i: (b, i, 0))
    const = lambda shape: pl.BlockSpec(shape, lambda b, i: (0,) * len(shape))
    return pl.pallas_call(
        functools.partial(_out_proj_kernel, final_norm=final_norm),
        grid=(bsz, s_len // tm),
        in_specs=[row(d_model), row(sb_out.shape[-1]), row(ml_out.shape[-1]),
                  const(w_a.shape), const(w_b.shape), const((1, d_model))],
        out_specs=row(d_model),
        out_shape=jax.ShapeDtypeStruct(x.shape, jnp.float32),
        compiler_params=pltpu.CompilerParams(
            dimension_semantics=("arbitrary", "arbitrary"),
            vmem_limit_bytes=VMEM_LIMIT_BYTES),
        name="out_proj",
    )(x, sb_out, ml_out, w_a, w_b, final_g)


def kernel(x, norm_g, w_in, b_igate, b_fgate, conv_w, conv_b, head_norm_g, w_out, final_norm_g):
    depth = w_in.shape[0]
    d_model = x.shape[-1]
    sb_w = SB_HEADS * SB_HEAD_DIM
    ml_w = ML_HEADS * ML_HEAD_DIM
    n_main = 4 * sb_w + 5 * ml_w
    assert w_in.shape[2] == n_main + 2 * ML_HEADS
    h = x
    for layer in range(depth):
        w_pad = jnp.pad(w_in[layer], ((0, 0), (0, LANES - 2 * ML_HEADS))).astype(jnp.bfloat16)
        gate_bias = jnp.pad(jnp.concatenate([b_igate[layer], b_fgate[layer]]),
                            (0, LANES - 2 * ML_HEADS)).reshape(1, LANES)
        (sbq, sbk, sbv, sbz, mq, mk, mv, mo, mz, gates) = _in_proj(
            h, norm_g[layer].reshape(1, d_model), w_pad, gate_bias,
            conv_w[layer], conv_b[layer].reshape(1, 2 * ml_w), sb_w=sb_w, ml_w=ml_w)
        sb_out = _sb_attn(sbq, sbk, sbv, sbz)
        ml_out = _mlstm(mq, mk, mv, mo, mz, gates, head_norm_g[layer].reshape(1, ml_w))
        w_o = w_out[layer].astype(jnp.bfloat16)
        h = _out_proj(h, sb_out, ml_out, w_o[:sb_w], w_o[sb_w:],
                      final_norm_g.reshape(1, d_model), final_norm=(layer == depth - 1))
    return h
```
